```python
import jax
import jax.numpy as jnp
from jax import lax
import numpy as np

D_MODEL = 1024
BATCH = 2
SEQ = 8192
DEPTH = 1

CONV_CH = D_MODEL
CONV_WIDTH = 31
GLA_HEADS = 4
GLA_DK = D_MODEL // 2 // GLA_HEADS
GLA_DV = D_MODEL // GLA_HEADS
GLA_KDIM = GLA_HEADS * GLA_DK
GLA_VDIM = GLA_HEADS * GLA_DV
GATE_RANK = 16
GATE_NORMALIZER = 16.0
GLA_CHUNK = 64
FFN_DIM = 2816
FFN_CONV_WIDTH = 3
EPS = 1e-6

IN_SIZES = (2 * CONV_CH, GLA_KDIM, GLA_KDIM, GLA_VDIM, GLA_VDIM, GATE_RANK, D_MODEL, D_MODEL)
IN_DIM = sum(IN_SIZES)

kernel_name = "hybrid_conformer_gla_gated_merge"


def _rmsnorm(x, g):
    xf = x.astype(jnp.float32)
    y = xf * lax.rsqrt(jnp.mean(xf * xf, axis=-1, keepdims=True) + EPS)
    return (y * g.astype(jnp.float32)).astype(x.dtype)


def _layernorm(x, g, b):
    xf = x.astype(jnp.float32)
    mu = jnp.mean(xf, axis=-1, keepdims=True)
    xc = xf - mu
    var = jnp.mean(xc * xc, axis=-1, keepdims=True)
    y = xc * lax.rsqrt(var + EPS)
    return (y * g.astype(jnp.float32) + b.astype(jnp.float32)).astype(x.dtype)


def _causal_dwconv(x, w, b):
    width = w.shape[0]
    y = lax.conv_general_dilated(
        x, w[:, None, :].astype(x.dtype), window_strides=(1,),
        padding=[(width - 1, 0)], dimension_numbers=("NWC", "WIO", "NWC"),
        feature_group_count=x.shape[-1])
    return y + b.astype(x.dtype)


def _split_cols(p):
    offs = np.cumsum(np.array(IN_SIZES))[:-1].tolist()
    return jnp.split(p, offs, axis=-1)


def _gla_chunked(q, k, v, gk):
    bsz, seq, nh, dk = q.shape
    dv = v.shape[-1]
    n = seq // GLA_CHUNK

    def blk(t):
        return t.reshape(bsz, n, GLA_CHUNK, nh, t.shape[-1]).transpose(0, 3, 1, 2, 4)

    q, k, v, gk = blk(q), blk(k), blk(v), blk(gk)
    b = jnp.cumsum(gk, axis=3)
    b_last = b[:, :, :, -1:, :]
    q_dec = q * jnp.exp(b)
    k_inv = k * jnp.exp(-b)
    k_end = k * jnp.exp(b_last - b)
    causal = jnp.tril(jnp.ones((GLA_CHUNK, GLA_CHUNK), dtype=bool))
    att = jnp.einsum("bhnid,bhnjd->bhnij", q_dec, k_inv)
    att = jnp.where(causal, att, 0.0)
    o_intra = jnp.einsum("bhnij,bhnjv->bhniv", att, v)
    kv_chunk = jnp.einsum("bhnjd,bhnjv->bhndv", k_end, v)
    decay = jnp.exp(b_last[:, :, :, 0, :])

    def step(state, inp):
        dec, kv = inp
        return dec[..., None] * state + kv, state

    init = jnp.zeros((bsz, nh, dk, dv), q.dtype)
    _, s_prev = lax.scan(step, init, (jnp.moveaxis(decay, 2, 0), jnp.moveaxis(kv_chunk, 2, 0)))
    s_prev = jnp.moveaxis(s_prev, 0, 2)
    o_inter = jnp.einsum("bhnid,bhndv->bhniv", q_dec, s_prev)
    o = o_intra + o_inter
    return o.transpose(0, 2, 3, 1, 4).reshape(bsz, seq, nh, dv)


def setup_inputs(seed: int = 0) -> dict:
    key = jax.random.key(seed)
    ks = jax.random.split(key, 20)
    f32 = jnp.float32
    L, D = DEPTH, D_MODEL

    def nrm(k, shape, scale):
        return jax.random.normal(k, shape, f32) * scale

    return {
        "x": jax.random.normal(ks[0], (BATCH, SEQ, D), f32),
        "norm1_g": 1.0 + nrm(ks[1], (L, D), 0.02),
        "w_in": nrm(ks[2], (L, D, IN_DIM), D ** -0.5),
        "conv_dw_w": nrm(ks[3], (L, CONV_WIDTH, CONV_CH), CONV_WIDTH ** -0.5),
        "conv_dw_b": nrm(ks[4], (L, CONV_CH), 0.02),
        "conv_ln_g": 1.0 + nrm(ks[5], (L, CONV_CH), 0.02),
        "conv_ln_b": nrm(ks[6], (L, CONV_CH), 0.02),
        "w_conv_out": nrm(ks[7], (L, CONV_CH, D), CONV_CH ** -0.5),
        "w_gate_up": nrm(ks[8], (L, GATE_RANK, GLA_KDIM), GATE_RANK ** -0.5),
        "b_gate": nrm(ks[9], (L, GLA_KDIM), 0.1),
        "gla_norm_g": 1.0 + nrm(ks[10], (L, GLA_DV), 0.02),
        "w_gla_out": nrm(ks[11], (L, GLA_VDIM, D), GLA_VDIM ** -0.5),
        "w_o": nrm(ks[12], (L, D, D), D ** -0.5),
        "norm2_g": 1.0 + nrm(ks[13], (L, D), 0.02),
        "w_ffn_up": nrm(ks[14], (L, D, 2 * FFN_DIM), D ** -0.5),
        "ffn_dw_w": nrm(ks[15], (L, FFN_CONV_WIDTH, 2 * FFN_DIM), FFN_CONV_WIDTH ** -0.5),
        "ffn_dw_b": nrm(ks[16], (L, 2 * FFN_DIM), 0.02),
        "w_ffn_down": nrm(ks[17], (L, FFN_DIM, D), FFN_DIM ** -0.5),
        "final_g": 1.0 + nrm(ks[18], (D,), 0.02),
    }


def reference(x, norm1_g, w_in, conv_dw_w, conv_dw_b, conv_ln_g, conv_ln_b, w_conv_out,
              w_gate_up, b_gate, gla_norm_g, w_gla_out, w_o, norm2_g, w_ffn_up,
              ffn_dw_w, ffn_dw_b, w_ffn_down, final_g):
    bsz, seq, _ = x.shape
    h = x
    for l in range(DEPTH):
        u = _rmsnorm(h, norm1_g[l])
        proj = u @ w_in[l]
        a_glu, q, k, v, r, g_low, gate_a, gate_b = _split_cols(proj)

        c = a_glu[..., :CONV_CH] * jax.nn.sigmoid(a_glu[..., CONV_CH:])
        c = _causal_dwconv(c, conv_dw_w[l], conv_dw_b[l])
        c = jax.nn.silu(_layernorm(c, conv_ln_g[l], conv_ln_b[l]))
        y_conv = c @ w_conv_out[l]

        gk = jax.nn.log_sigmoid((g_low @ w_gate_up[l] + b_gate[l]).astype(jnp.float32)) / GATE_NORMALIZER
        qh = q.astype(jnp.float32).reshape(bsz, seq, GLA_HEADS, GLA_DK) * (GLA_DK ** -0.5)
        kh = k.astype(jnp.float32).reshape(bsz, seq, GLA_HEADS, GLA_DK)
        vh = v.astype(jnp.float32).reshape(bsz, seq, GLA_HEADS, GLA_DV)
        gkh = gk.reshape(bsz, seq, GLA_HEADS, GLA_DK)
        o = _gla_chunked(qh, kh, vh, gkh)
        o = o * lax.rsqrt(jnp.mean(o * o, axis=-1, keepdims=True) + EPS) * gla_norm_g[l].astype(jnp.float32)
        o = o.reshape(bsz, seq, GLA_VDIM).astype(x.dtype) * jax.nn.silu(r)
        y_gla = o @ w_gla_out[l]

        merged = jax.nn.sigmoid(gate_a) * y_conv + jax.nn.sigmoid(gate_b) * y_gla
        h = h + merged @ w_o[l]

        u2 = _rmsnorm(h, norm2_g[l])
        z = _causal_dwconv(u2 @ w_ffn_up[l], ffn_dw_w[l], ffn_dw_b[l])
        h = h + (jax.nn.silu(z[..., :FFN_DIM]) * z[..., FFN_DIM:]) @ w_ffn_down[l]
    return _rmsnorm(h, final_g)
```

```python
import functools

import jax
import jax.numpy as jnp
import numpy as np
from jax import lax
from jax.experimental import pallas as pl
from jax.experimental.pallas import tpu as pltpu

D_MODEL = 1024
CONV_CH = D_MODEL
CONV_WIDTH = 31
GLA_HEADS = 4
GLA_DK = 128
GLA_DV = 256
GLA_KDIM = GLA_HEADS * GLA_DK
GLA_VDIM = GLA_HEADS * GLA_DV
GATE_RANK = 16
GATE_NORMALIZER = 16.0
GLA_CHUNK = 64
FFN_DIM = 2816
FFN_CONV_WIDTH = 3
EPS = 1e-6

LANES = 128
SUBLANES = 8
RANK_PAD = LANES

OFF_GLU_A = 0
OFF_GLU_G = OFF_GLU_A + CONV_CH
OFF_Q = OFF_GLU_G + CONV_CH
OFF_K = OFF_Q + GLA_KDIM
OFF_V = OFF_K + GLA_KDIM
OFF_R = OFF_V + GLA_VDIM
OFF_GATE_A = OFF_R + GLA_VDIM
OFF_GATE_B = OFF_GATE_A + D_MODEL
OFF_GLOW = OFF_GATE_B + D_MODEL
IN_DIM_PAD = OFF_GLOW + RANK_PAD

CONV_HALO = 32
FFN_HALO = SUBLANES

TILE_MIX = 256
TILE_FFN = 256
VMEM_LIMIT = 56 * 1024 * 1024

_BF16 = jnp.bfloat16
_F32 = jnp.float32


def _mm(a, b):
    return jnp.dot(a, b, preferred_element_type=_F32)


def _mm_nt(a, b):
    return lax.dot_general(a, b, (((1,), (1,)), ((), ())), preferred_element_type=_F32)


def _mm_tn(a, b):
    return lax.dot_general(a, b, (((0,), (0,)), ((), ())), preferred_element_type=_F32)


def _sigmoid(x):
    return 1.0 / (1.0 + jnp.exp(-x))


def _silu(x):
    return x * _sigmoid(x)


def _log_sigmoid(x):
    return jnp.minimum(x, 0.0) - jnp.log1p(jnp.exp(-jnp.abs(x)))


def _rmsnorm(x, g):
    return x * lax.rsqrt(jnp.mean(x * x, axis=-1, keepdims=True) + EPS) * g


def _split3_bf16(x):
    hi = x.astype(_BF16)
    r1 = x - hi.astype(_F32)
    mid = r1.astype(_BF16)
    lo = (r1 - mid.astype(_F32)).astype(_BF16)
    return hi, mid, lo


def _mixer_kernel(x_ref, n1g_ref, win_ref, cw_ref, cb_ref, lng_ref, lnb_ref, wco_ref,
                  wgu_ref, bg_ref, gng_ref, wgo_ref, wo_ref, tri_ref,
                  out_ref, cext_ref, st_ref, obuf_ref):
    tm = x_ref.shape[0]

    @pl.when(pl.program_id(1) == 0)
    def _():
        cext_ref[0:CONV_HALO, :] = jnp.zeros((CONV_HALO, CONV_CH), _F32)
        st_ref[...] = jnp.zeros(st_ref.shape, _F32)

    x = x_ref[...]
    u = _rmsnorm(x, n1g_ref[...]).astype(_BF16)

    def proj(off, width):
        return _mm(u, win_ref[:, off:off + width])

    c = proj(OFF_GLU_A, CONV_CH) * _sigmoid(proj(OFF_GLU_G, CONV_CH))
    cext_ref[CONV_HALO:CONV_HALO + tm, :] = c
    acc = jnp.broadcast_to(cb_ref[...], (tm, CONV_CH))
    base = CONV_HALO - (CONV_WIDTH - 1)
    for j in range(CONV_WIDTH):
        acc = acc + cw_ref[j:j + 1, :] * cext_ref[base + j:base + j + tm, :]
    cext_ref[0:CONV_HALO, :] = cext_ref[tm:tm + CONV_HALO, :]
    mu = jnp.mean(acc, axis=-1, keepdims=True)
    xc = acc - mu
    var = jnp.mean(xc * xc, axis=-1, keepdims=True)
    cn = xc * lax.rsqrt(var + EPS) * lng_ref[...] + lnb_ref[...]
    y_conv = _mm(_silu(cn).astype(_BF16), wco_ref[...])

    g_low = proj(OFF_GLOW, RANK_PAD).astype(_BF16)
    gk = _log_sigmoid(_mm(g_low, wgu_ref[...]) + bg_ref[...]) * (1.0 / GATE_NORMALIZER)
    tri = tri_ref[...]
    g_hi, g_mid, g_lo = _split3_bf16(gk)
    b = _mm(tri, g_hi) + _mm(tri, g_mid) + _mm(tri, g_lo)
    q = proj(OFF_Q, GLA_KDIM)
    k = proj(OFF_K, GLA_KDIM)
    v = proj(OFF_V, GLA_VDIM).astype(_BF16)
    r = proj(OFF_R, GLA_VDIM)
    row = lax.broadcasted_iota(jnp.int32, (GLA_CHUNK, GLA_CHUNK), 0)
    col = lax.broadcasted_iota(jnp.int32, (GLA_CHUNK, GLA_CHUNK), 1)
    causal = col <= row
    gng = gng_ref[...]
    scale = GLA_DK ** -0.5
    for ci in range(tm // GLA_CHUNK):
        rows = slice(ci * GLA_CHUNK, (ci + 1) * GLA_CHUNK)
        b_c = b[rows]
        b_last = b_c[GLA_CHUNK - 1:GLA_CHUNK]
        q_dec = (q[rows] * scale * jnp.exp(b_c)).astype(_BF16)
        k_inv = (k[rows] * jnp.exp(-b_c)).astype(_BF16)
        k_end = (k[rows] * jnp.exp(b_last - b_c)).astype(_BF16)
        decay = jnp.exp(b_last)
        v_c = v[rows]
        outs = []
        for h in range(GLA_HEADS):
            ks = slice(h * GLA_DK, (h + 1) * GLA_DK)
            vs = slice(h * GLA_DV, (h + 1) * GLA_DV)
            att = jnp.where(causal, _mm_nt(q_dec[:, ks], k_inv[:, ks]), 0.0).astype(_BF16)
            st = st_ref[h]
            o = _mm(att, v_c[:, vs]) + _mm_nt(q_dec[:, ks], st.astype(_BF16))
            st_ref[h] = decay[:, ks] * st + _mm_tn(v_c[:, vs], k_end[:, ks])
            outs.append(_rmsnorm(o, gng))
        o_c = jnp.concatenate(outs, axis=1) * _silu(r[rows])
        obuf_ref[rows, :] = o_c.astype(_BF16)
    y_gla = _mm(obuf_ref[...], wgo_ref[...])

    merged = (_sigmoid(proj(OFF_GATE_A, D_MODEL)) * y_conv
              + _sigmoid(proj(OFF_GATE_B, D_MODEL)) * y_gla)
    out_ref[...] = x + _mm(merged.astype(_BF16), wo_ref[...])


def _ffn_kernel(h_ref, n2g_ref, wup_ref, fw_ref, fb_ref, wdn_ref, fg_ref,
                out_ref, zext_ref):
    tm = h_ref.shape[0]

    @pl.when(pl.program_id(1) == 0)
    def _():
        zext_ref[0:FFN_HALO, :] = jnp.zeros((FFN_HALO, 2 * FFN_DIM), _F32)

    h = h_ref[...]
    u = _rmsnorm(h, n2g_ref[...]).astype(_BF16)
    zext_ref[FFN_HALO:FFN_HALO + tm, :] = _mm(u, wup_ref[...])
    base = FFN_HALO - (FFN_CONV_WIDTH - 1)
    z = jnp.broadcast_to(fb_ref[...], (tm, 2 * FFN_DIM))
    for j in range(FFN_CONV_WIDTH):
        z = z + fw_ref[j:j + 1, :] * zext_ref[base + j:base + j + tm, :]
    zext_ref[0:FFN_HALO, :] = zext_ref[tm:tm + FFN_HALO, :]
    act = (_silu(z[:, :FFN_DIM]) * z[:, FFN_DIM:]).astype(_BF16)
    h2 = h + _mm(act, wdn_ref[...])
    out_ref[...] = _rmsnorm(h2, fg_ref[...])


def _resident(shape):
    zeros = (0,) * len(shape)
    return pl.BlockSpec(shape, lambda b, s: zeros, pipeline_mode=pl.Buffered(1))


def _chunk_tri(tile):
    idx = np.arange(tile)
    same = (idx[:, None] // GLA_CHUNK) == (idx[None, :] // GLA_CHUNK)
    return jnp.asarray((same & (idx[None, :] <= idx[:, None])).astype(np.float32), dtype=_BF16)


def kernel(x, norm1_g, w_in, conv_dw_w, conv_dw_b, conv_ln_g, conv_ln_b, w_conv_out, w_gate_up, b_gate, gla_norm_g, w_gla_out, w_o, norm2_g, w_ffn_up, ffn_dw_w, ffn_dw_b, w_ffn_down, final_g):
    bsz, seq, d = x.shape
    assert d == D_MODEL and seq % TILE_MIX == 0 and seq % TILE_FFN == 0
    assert norm1_g.shape[0] == 1, "single layer"
    l = 0
    glow0 = OFF_R + GLA_VDIM
    w = w_in[l]
    w_in_r = jnp.concatenate(
        [w[:, :glow0], w[:, glow0 + GATE_RANK:],
         jnp.pad(w[:, glow0:glow0 + GATE_RANK], ((0, 0), (0, RANK_PAD - GATE_RANK)))],
        axis=1).astype(_BF16)
    w_gu = jnp.pad(w_gate_up[l], ((0, RANK_PAD - GATE_RANK), (0, 0))).astype(_BF16)
    row2 = lambda a: a.reshape(1, -1)

    tile_spec = lambda t: pl.BlockSpec((None, t, D_MODEL), lambda b, s: (b, s, 0))
    params = pltpu.CompilerParams(dimension_semantics=("arbitrary", "arbitrary"),
                                  vmem_limit_bytes=VMEM_LIMIT)

    mix_args = (
        row2(norm1_g[l]), w_in_r, conv_dw_w[l], row2(conv_dw_b[l]), row2(conv_ln_g[l]),
        row2(conv_ln_b[l]), w_conv_out[l].astype(_BF16), w_gu, row2(b_gate[l]),
        row2(gla_norm_g[l]), w_gla_out[l].astype(_BF16), w_o[l].astype(_BF16),
        _chunk_tri(TILE_MIX))
    h1 = pl.pallas_call(
        _mixer_kernel,
        grid=(bsz, seq // TILE_MIX),
        in_specs=[tile_spec(TILE_MIX)] + [_resident(a.shape) for a in mix_args],
        out_specs=tile_spec(TILE_MIX),
        out_shape=jax.ShapeDtypeStruct((bsz, seq, D_MODEL), _F32),
        scratch_shapes=[
            pltpu.VMEM((CONV_HALO + TILE_MIX, CONV_CH), _F32),
            pltpu.VMEM((GLA_HEADS, GLA_DV, GLA_DK), _F32),
            pltpu.VMEM((TILE_MIX, GLA_VDIM), _BF16),
        ],
        compiler_params=params,
        name="token_mixer",
    )(x, *mix_args)

    ffn_args = (
        row2(norm2_g[l]), w_ffn_up[l].astype(_BF16), ffn_dw_w[l], row2(ffn_dw_b[l]),
        w_ffn_down[l].astype(_BF16), row2(final_g))
    return pl.pallas_call(
        _ffn_kernel,
        grid=(bsz, seq // TILE_FFN),
        in_specs=[tile_spec(TILE_FFN)] + [_resident(a.shape) for a in ffn_args],
        out_specs=tile_spec(TILE_FFN),
        out_shape=jax.ShapeDtypeStruct((bsz, seq, D_MODEL), _F32),
        scratch_shapes=[pltpu.VMEM((FFN_HALO + TILE_FFN, 2 * FFN_DIM), _F32)],
        compiler_params=params,
        name="channel_mixer",
    )(h1, *ffn_args)
```

```python
import functools

import jax
import jax.numpy as jnp
import numpy as np
from jax import lax
from jax.experimental import pallas as pl
from jax.experimental.pallas import tpu as pltpu

D_MODEL = 1024
CONV_CH = D_MODEL
CONV_WIDTH = 31
GLA_HEADS = 4
GLA_DK = 128
GLA_DV = 256
GLA_KDIM = GLA_HEADS * GLA_DK
GLA_VDIM = GLA_HEADS * GLA_DV
GATE_RANK = 16
GATE_NORMALIZER = 16.0
GLA_CHUNK = 64
FFN_DIM = 2816
FFN_CONV_WIDTH = 3
EPS = 1e-6

LANES = 128
SUBLANES = 8
RANK_PAD = LANES

OFF_GLU_A = 0
OFF_GLU_G = OFF_GLU_A + CONV_CH
OFF_Q = OFF_GLU_G + CONV_CH
OFF_K = OFF_Q + GLA_KDIM
OFF_V = OFF_K + GLA_KDIM
OFF_R = OFF_V + GLA_VDIM
OFF_GATE_A = OFF_R + GLA_VDIM
OFF_GATE_B = OFF_GATE_A + D_MODEL
OFF_GLOW = OFF_GATE_B + D_MODEL
IN_DIM_PAD = OFF_GLOW + RANK_PAD

CONV_HALO = 32
FFN_HALO = SUBLANES

TILE_MIX = 256
CONV_COLS = 256
CONV_ROWS = 128
TILE_FFN = 256
VMEM_LIMIT = 56 * 1024 * 1024

_BF16 = jnp.bfloat16
_F32 = jnp.float32


def _mm(a, b):
    return jnp.dot(a, b, preferred_element_type=_F32)


def _mm_nt(a, b):
    return lax.dot_general(a, b, (((1,), (1,)), ((), ())), preferred_element_type=_F32)


def _mm_tn(a, b):
    return lax.dot_general(a, b, (((0,), (0,)), ((), ())), preferred_element_type=_F32)


def _sigmoid(x):
    return 1.0 / (1.0 + jnp.exp(-x))


def _silu(x):
    return x * _sigmoid(x)


def _log_sigmoid(x):
    return jnp.minimum(x, 0.0) - jnp.log1p(jnp.exp(-jnp.abs(x)))


def _rmsnorm(x, g):
    return x * lax.rsqrt(jnp.mean(x * x, axis=-1, keepdims=True) + EPS) * g


def _split3_bf16(x):
    hi = x.astype(_BF16)
    r1 = x - hi.astype(_F32)
    mid = r1.astype(_BF16)
    lo = (r1 - mid.astype(_F32)).astype(_BF16)
    return hi, mid, lo


def _conv_taps_by_phase():
    base = CONV_HALO - (CONV_WIDTH - 1)
    groups = {}
    for j in range(CONV_WIDTH):
        groups.setdefault((base + j) % SUBLANES, []).append(j)
    return base, sorted(groups.items())


def _conv_rows(cext_ref, cw_ref, cb_ref, conv_ref, lanes, r0):
    base, groups = _conv_taps_by_phase()
    out = None
    for phase, taps in groups:
        extra = SUBLANES if phase else 0
        z = None
        for j in taps:
            a0 = r0 + base + j - phase
            term = cw_ref[j:j + 1, lanes] * cext_ref[a0:a0 + CONV_ROWS + extra, lanes]
            z = term if z is None else z + term
        zs = z[phase:phase + CONV_ROWS] if phase else z
        out = zs if out is None else out + zs
    conv_ref[r0:r0 + CONV_ROWS, lanes] = out + cb_ref[:, lanes]


def _mixer_kernel(x_ref, xp_ref, n1g_ref, win_ref, cw_ref, cb_ref, lng_ref, lnb_ref, wco_ref,
                  wgu_ref, bg_ref, gng_ref, wgo_ref, wo_ref, tri_ref,
                  out_ref, cext_ref, conv_ref, st_ref, obuf_ref, u_ref, *, tiles_per_seq):
    tm = x_ref.shape[0]
    step = pl.program_id(0)

    @pl.when(step == 0)
    def _():
        conv_ref[...] = jnp.zeros(conv_ref.shape, _F32)
        obuf_ref[...] = jnp.zeros(obuf_ref.shape, _BF16)
        u_ref[...] = jnp.zeros(u_ref.shape, _BF16)

    @pl.when(step % tiles_per_seq == 0)
    def _():
        cext_ref[0:CONV_HALO, :] = jnp.zeros((CONV_HALO, CONV_CH), _F32)
        st_ref[...] = jnp.zeros(st_ref.shape, _F32)

    u = _rmsnorm(x_ref[...], n1g_ref[...]).astype(_BF16)
    u_prev = u_ref[...]

    def proj(lhs, off, width):
        return _mm(lhs, win_ref[:, off:off + width])

    def glu_block(cb):
        cols = slice(cb * CONV_COLS, (cb + 1) * CONV_COLS)
        a = proj(u, OFF_GLU_A + cb * CONV_COLS, CONV_COLS)
        g = proj(u, OFF_GLU_G + cb * CONV_COLS, CONV_COLS)
        cext_ref[CONV_HALO:CONV_HALO + tm, cols] = a * _sigmoid(g)

    conv_pieces = []
    for cb in range(CONV_CH // CONV_COLS):
        for lt in range(CONV_COLS // LANES):
            l0 = cb * CONV_COLS + lt * LANES
            for rb in range(tm // CONV_ROWS):
                conv_pieces.append((slice(l0, l0 + LANES), rb))

    def conv_piece(i):
        lanes, rb = conv_pieces[i]
        _conv_rows(cext_ref, cw_ref, cb_ref, conv_ref, lanes, rb * CONV_ROWS)
        if rb == tm // CONV_ROWS - 1:
            cext_ref[0:CONV_HALO, lanes] = cext_ref[tm:tm + CONV_HALO, lanes]

    y_gla = _mm(obuf_ref[...], wgo_ref[...])
    gate_a = _sigmoid(proj(u_prev, OFF_GATE_A, D_MODEL))
    gate_b = _sigmoid(proj(u_prev, OFF_GATE_B, D_MODEL))

    acc = conv_ref[...]
    mu = jnp.mean(acc, axis=-1, keepdims=True)
    xc = acc - mu
    var = jnp.mean(xc * xc, axis=-1, keepdims=True)
    cn = _silu(xc * lax.rsqrt(var + EPS) * lng_ref[...] + lnb_ref[...]).astype(_BF16)

    g_low = proj(u, OFF_GLOW, RANK_PAD).astype(_BF16)
    glu_block(0)
    glu_block(1)
    gk_pre = _mm(g_low, wgu_ref[...]) + bg_ref[...]
    q = proj(u, OFF_Q, GLA_KDIM)
    k = proj(u, OFF_K, GLA_KDIM)
    for i in range(0, 4):
        conv_piece(i)
    v = proj(u, OFF_V, GLA_VDIM).astype(_BF16)
    glu_block(2)
    for i in range(4, 8):
        conv_piece(i)
    glu_block(3)
    r = proj(u, OFF_R, GLA_VDIM)

    gk = _log_sigmoid(gk_pre) * (1.0 / GATE_NORMALIZER)
    tri = tri_ref[...]
    g_hi, g_mid, g_lo = _split3_bf16(gk)
    b = _mm(tri, g_hi) + _mm(tri, g_mid) + _mm(tri, g_lo)

    y_conv = _mm(cn, wco_ref[...])
    for i in range(8, 12):
        conv_piece(i)

    n_chunks = tm // GLA_CHUNK
    row = lax.broadcasted_iota(jnp.int32, (GLA_CHUNK, GLA_CHUNK), 0)
    col = lax.broadcasted_iota(jnp.int32, (GLA_CHUNK, GLA_CHUNK), 1)
    causal = col <= row
    gng = gng_ref[...]
    scale = GLA_DK ** -0.5
    q_dec, k_inv, k_end, decay = [], [], [], []
    for ci in range(n_chunks):
        rows = slice(ci * GLA_CHUNK, (ci + 1) * GLA_CHUNK)
        b_c = b[rows]
        b_last = b_c[GLA_CHUNK - 1:GLA_CHUNK]
        q_dec.append((q[rows] * scale * jnp.exp(b_c)).astype(_BF16))
        k_inv.append((k[rows] * jnp.exp(-b_c)).astype(_BF16))
        k_end.append((k[rows] * jnp.exp(b_last - b_c)).astype(_BF16))
        decay.append(jnp.exp(b_last))
    head_k = [slice(h * GLA_DK, (h + 1) * GLA_DK) for h in range(GLA_HEADS)]
    head_v = [slice(h * GLA_DV, (h + 1) * GLA_DV) for h in range(GLA_HEADS)]
    units = [(ci, h) for ci in range(n_chunks) for h in range(GLA_HEADS)]
    att = {}
    for ci, h in units:
        s = _mm_nt(q_dec[ci][:, head_k[h]], k_inv[ci][:, head_k[h]])
        att[ci, h] = jnp.where(causal, s, 0.0).astype(_BF16)
    kv = {}
    for ci, h in units:
        rows = slice(ci * GLA_CHUNK, (ci + 1) * GLA_CHUNK)
        kv[ci, h] = _mm_tn(v[rows, head_v[h]], k_end[ci][:, head_k[h]])

    merged = (gate_a * y_conv + gate_b * y_gla).astype(_BF16)
    out_ref[...] = xp_ref[...] + _mm(merged, wo_ref[...])

    state = {}
    for h in range(GLA_HEADS):
        state[0, h] = st_ref[h]
        for ci in range(n_chunks):
            state[ci + 1, h] = decay[ci][:, head_k[h]] * state[ci, h] + kv[ci, h]
        st_ref[h] = state[n_chunks, h]
    for i in range(12, 16):
        conv_piece(i)
    for ci in range(n_chunks):
        rows = slice(ci * GLA_CHUNK, (ci + 1) * GLA_CHUNK)
        outs = []
        for h in range(GLA_HEADS):
            o = (_mm(att[ci, h], v[rows, head_v[h]])
                 + _mm_nt(q_dec[ci][:, head_k[h]], state[ci, h].astype(_BF16)))
            outs.append(_rmsnorm(o, gng))
        o_c = jnp.concatenate(outs, axis=1) * _silu(r[rows])
        obuf_ref[rows, :] = o_c.astype(_BF16)
    u_ref[...] = u


def _ffn_kernel(h_ref, n2g_ref, wup_ref, fw_ref, fb_ref, wdn_ref, fg_ref,
                out_ref, zext_ref):
    tm = h_ref.shape[0]

    @pl.when(pl.program_id(1) == 0)
    def _():
        zext_ref[0:FFN_HALO, :] = jnp.zeros((FFN_HALO, 2 * FFN_DIM), _F32)

    h = h_ref[...]
    u = _rmsnorm(h, n2g_ref[...]).astype(_BF16)
    zext_ref[FFN_HALO:FFN_HALO + tm, :] = _mm(u, wup_ref[...])
    base = FFN_HALO - (FFN_CONV_WIDTH - 1)
    z = jnp.broadcast_to(fb_ref[...], (tm, 2 * FFN_DIM))
    for j in range(FFN_CONV_WIDTH):
        z = z + fw_ref[j:j + 1, :] * zext_ref[base + j:base + j + tm, :]
    zext_ref[0:FFN_HALO, :] = zext_ref[tm:tm + FFN_HALO, :]
    act = (_silu(z[:, :FFN_DIM]) * z[:, FFN_DIM:]).astype(_BF16)
    h2 = h + _mm(act, wdn_ref[...])
    out_ref[...] = _rmsnorm(h2, fg_ref[...])


def _resident(shape, grid_rank):
    zeros = (0,) * len(shape)
    if grid_rank == 1:
        return pl.BlockSpec(shape, lambda s: zeros, pipeline_mode=pl.Buffered(1))
    return pl.BlockSpec(shape, lambda b, s: zeros, pipeline_mode=pl.Buffered(1))


def _chunk_tri(tile):
    idx = np.arange(tile)
    same = (idx[:, None] // GLA_CHUNK) == (idx[None, :] // GLA_CHUNK)
    return jnp.asarray((same & (idx[None, :] <= idx[:, None])).astype(np.float32), dtype=_BF16)


def kernel(x, norm1_g, w_in, conv_dw_w, conv_dw_b, conv_ln_g, conv_ln_b, w_conv_out, w_gate_up, b_gate, gla_norm_g, w_gla_out, w_o, norm2_g, w_ffn_up, ffn_dw_w, ffn_dw_b, w_ffn_down, final_g):
    bsz, seq, d = x.shape
    assert d == D_MODEL and seq % TILE_MIX == 0 and seq % TILE_FFN == 0
    assert norm1_g.shape[0] == 1, "single layer"
    l = 0
    glow0 = OFF_R + GLA_VDIM
    w = w_in[l]
    w_in_r = jnp.concatenate(
        [w[:, :glow0], w[:, glow0 + GATE_RANK:],
         jnp.pad(w[:, glow0:glow0 + GATE_RANK], ((0, 0), (0, RANK_PAD - GATE_RANK)))],
        axis=1).astype(_BF16)
    w_gu = jnp.pad(w_gate_up[l], ((0, RANK_PAD - GATE_RANK), (0, 0))).astype(_BF16)
    row2 = lambda a: a.reshape(1, -1)

    tiles_per_seq = seq // TILE_MIX
    n_tiles = bsz * tiles_per_seq
    x2 = x.reshape(bsz * seq, D_MODEL)
    mix_args = (
        row2(norm1_g[l]), w_in_r, conv_dw_w[l], row2(conv_dw_b[l]), row2(conv_ln_g[l]),
        row2(conv_ln_b[l]), w_conv_out[l].astype(_BF16), w_gu, row2(b_gate[l]),
        row2(gla_norm_g[l]), w_gla_out[l].astype(_BF16), w_o[l].astype(_BF16),
        _chunk_tri(TILE_MIX))
    cur_spec = pl.BlockSpec((TILE_MIX, D_MODEL), lambda s: (jnp.minimum(s, n_tiles - 1), 0))
    prev_spec = pl.BlockSpec((TILE_MIX, D_MODEL), lambda s: (jnp.maximum(s - 1, 0), 0))
    h1 = pl.pallas_call(
        functools.partial(_mixer_kernel, tiles_per_seq=tiles_per_seq),
        grid=(n_tiles + 1,),
        in_specs=[cur_spec, prev_spec] + [_resident(a.shape, 1) for a in mix_args],
        out_specs=prev_spec,
        out_shape=jax.ShapeDtypeStruct((bsz * seq, D_MODEL), _F32),
        scratch_shapes=[
            pltpu.VMEM((CONV_HALO + TILE_MIX, CONV_CH), _F32),
            pltpu.VMEM((TILE_MIX, CONV_CH), _F32),
            pltpu.VMEM((GLA_HEADS, GLA_DV, GLA_DK), _F32),
            pltpu.VMEM((TILE_MIX, GLA_VDIM), _BF16),
            pltpu.VMEM((TILE_MIX, D_MODEL), _BF16),
        ],
        compiler_params=pltpu.CompilerParams(dimension_semantics=("arbitrary",),
                                             vmem_limit_bytes=VMEM_LIMIT),
        name="token_mixer",
    )(x2, x2, *mix_args).reshape(bsz, seq, D_MODEL)

    tile_spec = lambda t: pl.BlockSpec((None, t, D_MODEL), lambda b, s: (b, s, 0))
    ffn_args = (
        row2(norm2_g[l]), w_ffn_up[l].astype(_BF16), ffn_dw_w[l], row2(ffn_dw_b[l]),
        w_ffn_down[l].astype(_BF16), row2(final_g))
    return pl.pallas_call(
        _ffn_kernel,
        grid=(bsz, seq // TILE_FFN),
        in_specs=[tile_spec(TILE_FFN)] + [_resident(a.shape, 2) for a in ffn_args],
        out_specs=tile_spec(TILE_FFN),
        out_shape=jax.ShapeDtypeStruct((bsz, seq, D_MODEL), _F32),
        scratch_shapes=[pltpu.VMEM((FFN_HALO + TILE_FFN, 2 * FFN_DIM), _F32)],
        compiler_params=pltpu.CompilerParams(dimension_semantics=("arbitrary", "arbitrary"),
                                             vmem_limit_bytes=VMEM_LIMIT),
        name="channel_mixer",
    )(h1, *ffn_args)
```

```python
import functools

import jax
import jax.numpy as jnp
import numpy as np
from jax import lax
from jax.experimental import pallas as pl
from jax.experimental.pallas import tpu as pltpu

D_MODEL = 1024
CONV_CH = D_MODEL
CONV_WIDTH = 31
GLA_HEADS = 4
GLA_DK = 128
GLA_DV = 256
GLA_KDIM = GLA_HEADS * GLA_DK
GLA_VDIM = GLA_HEADS * GLA_DV
GATE_RANK = 16
GATE_NORMALIZER = 16.0
GLA_CHUNK = 64
FFN_DIM = 2816
FFN_CONV_WIDTH = 3
EPS = 1e-6

LANES = 128
SUBLANES = 8
RANK_PAD = LANES

OFF_GLU_A = 0
OFF_GLU_G = OFF_GLU_A + CONV_CH
OFF_Q = OFF_GLU_G + CONV_CH
OFF_K = OFF_Q + GLA_KDIM
OFF_V = OFF_K + GLA_KDIM
OFF_R = OFF_V + GLA_VDIM
OFF_GATE_A = OFF_R + GLA_VDIM
OFF_GATE_B = OFF_GATE_A + D_MODEL
OFF_GLOW = OFF_GATE_B + D_MODEL
IN_DIM_PAD = OFF_GLOW + RANK_PAD

CONV_HALO = 32
FFN_HALO = SUBLANES

TILE_MIX = 256
CONV_COLS = 256
CONV_ROWS = 128
TILE_FFN = 512
VMEM_LIMIT = 56 * 1024 * 1024

_BF16 = jnp.bfloat16
_F32 = jnp.float32


def _mm(a, b):
    return jnp.dot(a, b, preferred_element_type=_F32)


def _mm_nt(a, b):
    return lax.dot_general(a, b, (((1,), (1,)), ((), ())), preferred_element_type=_F32)


def _mm_tn(a, b):
    return lax.dot_general(a, b, (((0,), (0,)), ((), ())), preferred_element_type=_F32)


def _sigmoid(x):
    return 1.0 / (1.0 + jnp.exp(-x))


def _silu(x):
    return x * _sigmoid(x)


def _log_sigmoid(x):
    return jnp.minimum(x, 0.0) - jnp.log1p(jnp.exp(-jnp.abs(x)))


def _rmsnorm(x, g):
    return x * lax.rsqrt(jnp.mean(x * x, axis=-1, keepdims=True) + EPS) * g


def _split3_bf16(x):
    hi = x.astype(_BF16)
    r1 = x - hi.astype(_F32)
    mid = r1.astype(_BF16)
    lo = (r1 - mid.astype(_F32)).astype(_BF16)
    return hi, mid, lo


def _conv_taps_by_phase():
    base = CONV_HALO - (CONV_WIDTH - 1)
    groups = {}
    for j in range(CONV_WIDTH):
        groups.setdefault((base + j) % SUBLANES, []).append(j)
    return base, sorted(groups.items())


def _conv_rows(cext_ref, cw_ref, cb_ref, conv_ref, lanes, r0):
    base, groups = _conv_taps_by_phase()
    out = None
    for phase, taps in groups:
        extra = SUBLANES if phase else 0
        z = None
        for j in taps:
            a0 = r0 + base + j - phase
            term = cw_ref[j:j + 1, lanes] * cext_ref[a0:a0 + CONV_ROWS + extra, lanes]
            z = term if z is None else z + term
        zs = z[phase:phase + CONV_ROWS] if phase else z
        out = zs if out is None else out + zs
    conv_ref[r0:r0 + CONV_ROWS, lanes] = out + cb_ref[:, lanes]


def _mixer_kernel(x_ref, xp_ref, n1g_ref, win_ref, cw_ref, cb_ref, lng_ref, lnb_ref, wco_ref,
                  wgu_ref, bg_ref, gng_ref, wgo_ref, wo_ref, tri_ref,
                  out_ref, cext_ref, conv_ref, st_ref, obuf_ref, u_ref, *, tiles_per_seq):
    tm = x_ref.shape[0]
    step = pl.program_id(0)

    @pl.when(step == 0)
    def _():
        conv_ref[...] = jnp.zeros(conv_ref.shape, _F32)
        obuf_ref[...] = jnp.zeros(obuf_ref.shape, _BF16)
        u_ref[...] = jnp.zeros(u_ref.shape, _BF16)

    @pl.when(step % tiles_per_seq == 0)
    def _():
        cext_ref[0:CONV_HALO, :] = jnp.zeros((CONV_HALO, CONV_CH), _F32)
        st_ref[...] = jnp.zeros(st_ref.shape, _F32)

    u = _rmsnorm(x_ref[...], n1g_ref[...]).astype(_BF16)
    u_prev = u_ref[...]

    def proj(lhs, off, width):
        return _mm(lhs, win_ref[:, off:off + width])

    def glu_block(cb):
        cols = slice(cb * CONV_COLS, (cb + 1) * CONV_COLS)
        a = proj(u, OFF_GLU_A + cb * CONV_COLS, CONV_COLS)
        g = proj(u, OFF_GLU_G + cb * CONV_COLS, CONV_COLS)
        cext_ref[CONV_HALO:CONV_HALO + tm, cols] = a * _sigmoid(g)

    conv_pieces = []
    for cb in range(CONV_CH // CONV_COLS):
        for lt in range(CONV_COLS // LANES):
            l0 = cb * CONV_COLS + lt * LANES
            for rb in range(tm // CONV_ROWS):
                conv_pieces.append((slice(l0, l0 + LANES), rb))

    def conv_piece(i):
        lanes, rb = conv_pieces[i]
        _conv_rows(cext_ref, cw_ref, cb_ref, conv_ref, lanes, rb * CONV_ROWS)
        if rb == tm // CONV_ROWS - 1:
            cext_ref[0:CONV_HALO, lanes] = cext_ref[tm:tm + CONV_HALO, lanes]

    y_gla = _mm(obuf_ref[...], wgo_ref[...])
    gate_a = _sigmoid(proj(u_prev, OFF_GATE_A, D_MODEL))
    gate_b = _sigmoid(proj(u_prev, OFF_GATE_B, D_MODEL))

    acc = conv_ref[...]
    mu = jnp.mean(acc, axis=-1, keepdims=True)
    xc = acc - mu
    var = jnp.mean(xc * xc, axis=-1, keepdims=True)
    cn = _silu(xc * lax.rsqrt(var + EPS) * lng_ref[...] + lnb_ref[...]).astype(_BF16)

    g_low = proj(u, OFF_GLOW, RANK_PAD).astype(_BF16)
    glu_block(0)
    glu_block(1)
    gk_pre = _mm(g_low, wgu_ref[...]) + bg_ref[...]
    q = proj(u, OFF_Q, GLA_KDIM)
    k = proj(u, OFF_K, GLA_KDIM)
    for i in range(0, 4):
        conv_piece(i)
    v = proj(u, OFF_V, GLA_VDIM).astype(_BF16)
    glu_block(2)
    for i in range(4, 8):
        conv_piece(i)
    glu_block(3)
    r = proj(u, OFF_R, GLA_VDIM)

    gk = _log_sigmoid(gk_pre) * (1.0 / GATE_NORMALIZER)
    tri = tri_ref[...]
    g_hi, g_mid, g_lo = _split3_bf16(gk)
    b = _mm(tri, g_hi) + _mm(tri, g_mid) + _mm(tri, g_lo)

    y_conv = _mm(cn, wco_ref[...])
    for i in range(8, 12):
        conv_piece(i)

    n_chunks = tm // GLA_CHUNK
    row = lax.broadcasted_iota(jnp.int32, (GLA_CHUNK, GLA_CHUNK), 0)
    col = lax.broadcasted_iota(jnp.int32, (GLA_CHUNK, GLA_CHUNK), 1)
    causal = col <= row
    gng = gng_ref[...]
    scale = GLA_DK ** -0.5
    q_dec, k_inv, k_end, decay = [], [], [], []
    for ci in range(n_chunks):
        rows = slice(ci * GLA_CHUNK, (ci + 1) * GLA_CHUNK)
        b_c = b[rows]
        b_last = b_c[GLA_CHUNK - 1:GLA_CHUNK]
        q_dec.append((q[rows] * scale * jnp.exp(b_c)).astype(_BF16))
        k_inv.append((k[rows] * jnp.exp(-b_c)).astype(_BF16))
        k_end.append((k[rows] * jnp.exp(b_last - b_c)).astype(_BF16))
        decay.append(jnp.exp(b_last))
    head_k = [slice(h * GLA_DK, (h + 1) * GLA_DK) for h in range(GLA_HEADS)]
    head_v = [slice(h * GLA_DV, (h + 1) * GLA_DV) for h in range(GLA_HEADS)]
    units = [(ci, h) for ci in range(n_chunks) for h in range(GLA_HEADS)]
    att = {}
    for ci, h in units:
        s = _mm_nt(q_dec[ci][:, head_k[h]], k_inv[ci][:, head_k[h]])
        att[ci, h] = jnp.where(causal, s, 0.0).astype(_BF16)
    kv = {}
    for ci, h in units:
        rows = slice(ci * GLA_CHUNK, (ci + 1) * GLA_CHUNK)
        kv[ci, h] = _mm_tn(v[rows, head_v[h]], k_end[ci][:, head_k[h]])

    merged = (gate_a * y_conv + gate_b * y_gla).astype(_BF16)
    out_ref[...] = xp_ref[...] + _mm(merged, wo_ref[...])

    state = {}
    for h in range(GLA_HEADS):
        state[0, h] = st_ref[h]
        for ci in range(n_chunks):
            state[ci + 1, h] = decay[ci][:, head_k[h]] * state[ci, h] + kv[ci, h]
        st_ref[h] = state[n_chunks, h]
    for i in range(12, 16):
        conv_piece(i)
    for ci in range(n_chunks):
        rows = slice(ci * GLA_CHUNK, (ci + 1) * GLA_CHUNK)
        outs = []
        for h in range(GLA_HEADS):
            o = (_mm(att[ci, h], v[rows, head_v[h]])
                 + _mm_nt(q_dec[ci][:, head_k[h]], state[ci, h].astype(_BF16)))
            outs.append(_rmsnorm(o, gng))
        o_c = jnp.concatenate(outs, axis=1) * _silu(r[rows])
        obuf_ref[rows, :] = o_c.astype(_BF16)
    u_ref[...] = u


def _ffn_kernel(h_ref, n2g_ref, wup_ref, fw_ref, fb_ref, wdn_ref, fg_ref,
                out_ref, zext_ref):
    tm = h_ref.shape[0]

    @pl.when(pl.program_id(1) == 0)
    def _():
        zext_ref[0:FFN_HALO, :] = jnp.zeros((FFN_HALO, 2 * FFN_DIM), _F32)

    h = h_ref[...]
    u = _rmsnorm(h, n2g_ref[...]).astype(_BF16)
    zext_ref[FFN_HALO:FFN_HALO + tm, :] = _mm(u, wup_ref[...])
    base = FFN_HALO - (FFN_CONV_WIDTH - 1)
    z = jnp.broadcast_to(fb_ref[...], (tm, 2 * FFN_DIM))
    for j in range(FFN_CONV_WIDTH):
        z = z + fw_ref[j:j + 1, :] * zext_ref[base + j:base + j + tm, :]
    zext_ref[0:FFN_HALO, :] = zext_ref[tm:tm + FFN_HALO, :]
    act = (_silu(z[:, :FFN_DIM]) * z[:, FFN_DIM:]).astype(_BF16)
    h2 = h + _mm(act, wdn_ref[...])
    out_ref[...] = _rmsnorm(h2, fg_ref[...])


def _resident(shape, grid_rank):
    zeros = (0,) * len(shape)
    if grid_rank == 1:
        return pl.BlockSpec(shape, lambda s: zeros, pipeline_mode=pl.Buffered(1))
    return pl.BlockSpec(shape, lambda b, s: zeros, pipeline_mode=pl.Buffered(1))


def _chunk_tri(tile):
    idx = np.arange(tile)
    same = (idx[:, None] // GLA_CHUNK) == (idx[None, :] // GLA_CHUNK)
    return jnp.asarray((same & (idx[None, :] <= idx[:, None])).astype(np.float32), dtype=_BF16)


def kernel(x, norm1_g, w_in, conv_dw_w, conv_dw_b, conv_ln_g, conv_ln_b, w_conv_out, w_gate_up, b_gate, gla_norm_g, w_gla_out, w_o, norm2_g, w_ffn_up, ffn_dw_w, ffn_dw_b, w_ffn_down, final_g):
    bsz, seq, d = x.shape
    assert d == D_MODEL and seq % TILE_MIX == 0 and seq % TILE_FFN == 0
    assert norm1_g.shape[0] == 1, "single layer"
    l = 0
    glow0 = OFF_R + GLA_VDIM
    w = w_in[l].astype(_BF16)
    w_in_r = jnp.concatenate(
        [w[:, :glow0], w[:, glow0 + GATE_RANK:],
         jnp.pad(w[:, glow0:glow0 + GATE_RANK], ((0, 0), (0, RANK_PAD - GATE_RANK)))],
        axis=1)
    w_gu = jnp.pad(w_gate_up[l], ((0, RANK_PAD - GATE_RANK), (0, 0))).astype(_BF16)
    row2 = lambda a: a.reshape(1, -1)

    tiles_per_seq = seq // TILE_MIX
    n_tiles = bsz * tiles_per_seq
    x2 = x.reshape(bsz * seq, D_MODEL)
    mix_args = (
        row2(norm1_g[l]), w_in_r, conv_dw_w[l], row2(conv_dw_b[l]), row2(conv_ln_g[l]),
        row2(conv_ln_b[l]), w_conv_out[l].astype(_BF16), w_gu, row2(b_gate[l]),
        row2(gla_norm_g[l]), w_gla_out[l].astype(_BF16), w_o[l].astype(_BF16),
        _chunk_tri(TILE_MIX))
    cur_spec = pl.BlockSpec((TILE_MIX, D_MODEL), lambda s: (jnp.minimum(s, n_tiles - 1), 0))
    prev_spec = pl.BlockSpec((TILE_MIX, D_MODEL), lambda s: (jnp.maximum(s - 1, 0), 0))
    h1 = pl.pallas_call(
        functools.partial(_mixer_kernel, tiles_per_seq=tiles_per_seq),
        grid=(n_tiles + 1,),
        in_specs=[cur_spec, prev_spec] + [_resident(a.shape, 1) for a in mix_args],
        out_specs=prev_spec,
        out_shape=jax.ShapeDtypeStruct((bsz * seq, D_MODEL), _F32),
        scratch_shapes=[
            pltpu.VMEM((CONV_HALO + TILE_MIX, CONV_CH), _F32),
            pltpu.VMEM((TILE_MIX, CONV_CH), _F32),
            pltpu.VMEM((GLA_HEADS, GLA_DV, GLA_DK), _F32),
            pltpu.VMEM((TILE_MIX, GLA_VDIM), _BF16),
            pltpu.VMEM((TILE_MIX, D_MODEL), _BF16),
        ],
        compiler_params=pltpu.CompilerParams(dimension_semantics=("arbitrary",),
                                             vmem_limit_bytes=VMEM_LIMIT),
        name="token_mixer",
    )(x2, x2, *mix_args).reshape(bsz, seq, D_MODEL)

    tile_spec = lambda t: pl.BlockSpec((None, t, D_MODEL), lambda b, s: (b, s, 0))
    ffn_args = (
        row2(norm2_g[l]), w_ffn_up[l].astype(_BF16), ffn_dw_w[l], row2(ffn_dw_b[l]),
        w_ffn_down[l].astype(_BF16), row2(final_g))
    return pl.pallas_call(
        _ffn_kernel,
        grid=(bsz, seq // TILE_FFN),
        in_specs=[tile_spec(TILE_FFN)] + [_resident(a.shape, 2) for a in ffn_args],
        out_specs=tile_spec(TILE_FFN),
        out_shape=jax.ShapeDtypeStruct((bsz, seq, D_MODEL), _F32),
        scratch_shapes=[pltpu.VMEM((FFN_HALO + TILE_FFN, 2 * FFN_DIM), _F32)],
        compiler_params=pltpu.CompilerParams(dimension_semantics=("arbitrary", "arbitrary"),
                                             vmem_limit_bytes=VMEM_LIMIT),
        name="channel_mixer",
    )(h1, *ffn_args)
```

```python
import functools

import jax
import jax.numpy as jnp
import numpy as np
from jax import lax
from jax.experimental import pallas as pl
from jax.experimental.pallas import tpu as pltpu

D_MODEL = 1024
CONV_CH = D_MODEL
CONV_WIDTH = 31
GLA_HEADS = 4
GLA_DK = 128
GLA_DV = 256
GLA_KDIM = GLA_HEADS * GLA_DK
GLA_VDIM = GLA_HEADS * GLA_DV
GATE_RANK = 16
GATE_NORMALIZER = 16.0
GLA_CHUNK = 64
FFN_DIM = 2816
FFN_CONV_WIDTH = 3
EPS = 1e-6

LANES = 128
SUBLANES = 8
RANK_PAD = LANES

OFF_GLU_A = 0
OFF_GLU_G = OFF_GLU_A + CONV_CH
OFF_Q = OFF_GLU_G + CONV_CH
OFF_K = OFF_Q + GLA_KDIM
OFF_V = OFF_K + GLA_KDIM
OFF_R = OFF_V + GLA_VDIM
OFF_GATE_A = OFF_R + GLA_VDIM
OFF_GATE_B = OFF_GATE_A + D_MODEL
OFF_GLOW = OFF_GATE_B + D_MODEL
IN_DIM_PAD = OFF_GLOW + RANK_PAD

CONV_HALO = 32
FFN_HALO = SUBLANES

TILE_MIX = 256
CONV_COLS = 256
N_LANE_TILES = CONV_CH // LANES
MXU_COLS = 256
CONV_ROWS = 128
TILE_FFN = 512
VMEM_LIMIT = 56 * 1024 * 1024

_BF16 = jnp.bfloat16
_F32 = jnp.float32


def _mm(a, b):
    return jnp.dot(a, b, preferred_element_type=_F32)


def _mm_nt(a, b):
    return lax.dot_general(a, b, (((1,), (1,)), ((), ())), preferred_element_type=_F32)


def _mm_tn(a, b):
    return lax.dot_general(a, b, (((0,), (0,)), ((), ())), preferred_element_type=_F32)


def _sigmoid(x):
    return 1.0 / (1.0 + jnp.exp(-x))


def _silu(x):
    return x * _sigmoid(x)


def _log_sigmoid(x):
    return jnp.minimum(x, 0.0) - jnp.log1p(jnp.exp(-jnp.abs(x)))


def _rmsnorm(x, g):
    return x * lax.rsqrt(jnp.mean(x * x, axis=-1, keepdims=True) + EPS) * g


def _split3_bf16(x):
    hi = x.astype(_BF16)
    r1 = x - hi.astype(_F32)
    mid = r1.astype(_BF16)
    lo = (r1 - mid.astype(_F32)).astype(_BF16)
    return hi, mid, lo


def _conv_taps_by_phase():
    base = CONV_HALO - (CONV_WIDTH - 1)
    groups = {}
    for j in range(CONV_WIDTH):
        groups.setdefault((base + j) % SUBLANES, []).append(j)
    return base, sorted(groups.items())


def _conv_lane_tile(cext_ref, cw_ref, cb_ref, conv_ref, lt, tm):
    base, groups = _conv_taps_by_phase()
    for r0 in range(0, tm, CONV_ROWS):
        out = None
        for phase, taps in groups:
            extra = SUBLANES if phase else 0
            z = None
            for j in taps:
                a0 = r0 + base + j - phase
                term = cw_ref[lt, j:j + 1, :] * cext_ref[lt, a0:a0 + CONV_ROWS + extra, :]
                z = term if z is None else z + term
            zs = z[phase:phase + CONV_ROWS] if phase else z
            out = zs if out is None else out + zs
        conv_ref[lt, r0:r0 + CONV_ROWS, :] = out + cb_ref[lt]
    cext_ref[lt, 0:CONV_HALO, :] = cext_ref[lt, tm:tm + CONV_HALO, :]


def _mixer_kernel(x_ref, xp_ref, n1g_ref, win_ref, cw_ref, cb_ref, lng_ref, lnb_ref, wco_ref,
                  wgu_ref, bg_ref, gng_ref, wgo_ref, wo_ref, tri_ref,
                  out_ref, cext_ref, conv_ref, st_ref, obuf_ref, u_ref, ucur_ref, proj_ref, gate_ref,
                  *, tiles_per_seq):
    tm = x_ref.shape[0]
    step = pl.program_id(0)

    @pl.when(step == 0)
    def _():
        cext_ref[...] = jnp.zeros(cext_ref.shape, _F32)
        obuf_ref[...] = jnp.zeros(obuf_ref.shape, _BF16)
        u_ref[...] = jnp.zeros(u_ref.shape, _BF16)

    @pl.when(step % tiles_per_seq == 0)
    def _():
        st_ref[...] = jnp.zeros(st_ref.shape, _F32)

    @pl.when(step % tiles_per_seq == 1)
    def _():
        cext_ref[:, 0:CONV_HALO, :] = jnp.zeros((N_LANE_TILES, CONV_HALO, LANES), _F32)

    ucur_ref[...] = _rmsnorm(x_ref[...], n1g_ref[...]).astype(_BF16)

    def weight_block(off, blk):
        return win_ref[:, pl.ds(pl.multiple_of(off + blk * MXU_COLS, MXU_COLS), MXU_COLS)]

    def front_body(i, carry):
        _conv_lane_tile(cext_ref, cw_ref, cb_ref, conv_ref, i, tm)
        for half in range(2):
            blk = 2 * i + half
            proj_ref[blk] = _mm(ucur_ref[...], weight_block(OFF_Q, blk))
        return carry

    n_front = (2 * GLA_KDIM + 2 * GLA_VDIM) // (2 * MXU_COLS)
    lax.fori_loop(0, n_front, front_body, 0)

    def gate_body(i, carry):
        _conv_lane_tile(cext_ref, cw_ref, cb_ref, conv_ref, n_front + i, tm)
        for half in range(4):
            blk = 4 * i + half
            gate_ref[blk] = _mm(u_ref[...], weight_block(OFF_GATE_A, blk))
        return carry

    lax.fori_loop(0, N_LANE_TILES - n_front, gate_body, 0)

    y_gla = _mm(obuf_ref[...], wgo_ref[...])

    u = ucur_ref[...]

    def proj(off, width):
        return _mm(u, win_ref[:, off:off + width])

    acc = jnp.concatenate([conv_ref[lt] for lt in range(N_LANE_TILES)], axis=1)
    mu = jnp.mean(acc, axis=-1, keepdims=True)
    xc = acc - mu
    var = jnp.mean(xc * xc, axis=-1, keepdims=True)
    cn = _silu(xc * lax.rsqrt(var + EPS) * lng_ref[...] + lnb_ref[...]).astype(_BF16)

    g_low = proj(OFF_GLOW, RANK_PAD).astype(_BF16)
    for cb in range(CONV_CH // CONV_COLS):
        a = proj(OFF_GLU_A + cb * CONV_COLS, CONV_COLS)
        g = proj(OFF_GLU_G + cb * CONV_COLS, CONV_COLS)
        c = a * _sigmoid(g)
        for lt in range(CONV_COLS // LANES):
            cext_ref[cb * (CONV_COLS // LANES) + lt, CONV_HALO:CONV_HALO + tm, :] = (
                c[:, lt * LANES:(lt + 1) * LANES])
    gk_pre = _mm(g_low, wgu_ref[...]) + bg_ref[...]
    y_conv = _mm(cn, wco_ref[...])

    gk = _log_sigmoid(gk_pre) * (1.0 / GATE_NORMALIZER)
    tri = tri_ref[...]
    g_hi, g_mid, g_lo = _split3_bf16(gk)
    b = _mm(tri, g_hi) + _mm(tri, g_mid) + _mm(tri, g_lo)

    n_gate = D_MODEL // MXU_COLS
    gate_a = _sigmoid(jnp.concatenate([gate_ref[i] for i in range(n_gate)], axis=1))
    gate_b = _sigmoid(jnp.concatenate([gate_ref[n_gate + i] for i in range(n_gate)], axis=1))
    merged = (gate_a * y_conv + gate_b * y_gla).astype(_BF16)
    out_ref[...] = xp_ref[...] + _mm(merged, wo_ref[...])

    def proj_cols(first, count):
        return jnp.concatenate([proj_ref[first + i] for i in range(count)], axis=1)

    kb = GLA_KDIM // MXU_COLS
    vb = GLA_VDIM // MXU_COLS
    q = proj_cols(0, kb)
    k = proj_cols(kb, kb)
    v = proj_cols(2 * kb, vb).astype(_BF16)
    r = proj_cols(2 * kb + vb, vb)

    n_chunks = tm // GLA_CHUNK
    row = lax.broadcasted_iota(jnp.int32, (GLA_CHUNK, GLA_CHUNK), 0)
    col = lax.broadcasted_iota(jnp.int32, (GLA_CHUNK, GLA_CHUNK), 1)
    causal = col <= row
    gng = gng_ref[...]
    scale = GLA_DK ** -0.5
    q_dec, k_inv, k_end, decay = [], [], [], []
    for ci in range(n_chunks):
        rows = slice(ci * GLA_CHUNK, (ci + 1) * GLA_CHUNK)
        b_c = b[rows]
        b_last = b_c[GLA_CHUNK - 1:GLA_CHUNK]
        q_dec.append((q[rows] * scale * jnp.exp(b_c)).astype(_BF16))
        k_inv.append((k[rows] * jnp.exp(-b_c)).astype(_BF16))
        k_end.append((k[rows] * jnp.exp(b_last - b_c)).astype(_BF16))
        decay.append(jnp.exp(b_last))
    head_k = [slice(h * GLA_DK, (h + 1) * GLA_DK) for h in range(GLA_HEADS)]
    head_v = [slice(h * GLA_DV, (h + 1) * GLA_DV) for h in range(GLA_HEADS)]
    units = [(ci, h) for ci in range(n_chunks) for h in range(GLA_HEADS)]
    att = {}
    for ci, h in units:
        s = _mm_nt(q_dec[ci][:, head_k[h]], k_inv[ci][:, head_k[h]])
        att[ci, h] = jnp.where(causal, s, 0.0).astype(_BF16)
    kv = {}
    for ci, h in units:
        rows = slice(ci * GLA_CHUNK, (ci + 1) * GLA_CHUNK)
        kv[ci, h] = _mm_tn(v[rows, head_v[h]], k_end[ci][:, head_k[h]])
    state = {}
    for h in range(GLA_HEADS):
        state[0, h] = st_ref[h]
        for ci in range(n_chunks):
            state[ci + 1, h] = decay[ci][:, head_k[h]] * state[ci, h] + kv[ci, h]
        st_ref[h] = state[n_chunks, h]
    for ci in range(n_chunks):
        rows = slice(ci * GLA_CHUNK, (ci + 1) * GLA_CHUNK)
        outs = []
        for h in range(GLA_HEADS):
            o = (_mm(att[ci, h], v[rows, head_v[h]])
                 + _mm_nt(q_dec[ci][:, head_k[h]], state[ci, h].astype(_BF16)))
            outs.append(_rmsnorm(o, gng))
        o_c = jnp.concatenate(outs, axis=1) * _silu(r[rows])
        obuf_ref[rows, :] = o_c.astype(_BF16)
    u_ref[...] = u


def _ffn_kernel(h_ref, n2g_ref, wup_ref, fw_ref, fb_ref, wdn_ref, fg_ref,
                out_ref, zext_ref):
    tm = h_ref.shape[0]

    @pl.when(pl.program_id(1) == 0)
    def _():
        zext_ref[0:FFN_HALO, :] = jnp.zeros((FFN_HALO, 2 * FFN_DIM), _F32)

    h = h_ref[...]
    u = _rmsnorm(h, n2g_ref[...]).astype(_BF16)
    zext_ref[FFN_HALO:FFN_HALO + tm, :] = _mm(u, wup_ref[...])
    base = FFN_HALO - (FFN_CONV_WIDTH - 1)
    z = jnp.broadcast_to(fb_ref[...], (tm, 2 * FFN_DIM))
    for j in range(FFN_CONV_WIDTH):
        z = z + fw_ref[j:j + 1, :] * zext_ref[base + j:base + j + tm, :]
    zext_ref[0:FFN_HALO, :] = zext_ref[tm:tm + FFN_HALO, :]
    act = (_silu(z[:, :FFN_DIM]) * z[:, FFN_DIM:]).astype(_BF16)
    h2 = h + _mm(act, wdn_ref[...])
    out_ref[...] = _rmsnorm(h2, fg_ref[...])


def _resident(shape, grid_rank):
    zeros = (0,) * len(shape)
    if grid_rank == 1:
        return pl.BlockSpec(shape, lambda s: zeros, pipeline_mode=pl.Buffered(1))
    return pl.BlockSpec(shape, lambda b, s: zeros, pipeline_mode=pl.Buffered(1))


def _chunk_tri(tile):
    idx = np.arange(tile)
    same = (idx[:, None] // GLA_CHUNK) == (idx[None, :] // GLA_CHUNK)
    return jnp.asarray((same & (idx[None, :] <= idx[:, None])).astype(np.float32), dtype=_BF16)


def kernel(x, norm1_g, w_in, conv_dw_w, conv_dw_b, conv_ln_g, conv_ln_b, w_conv_out, w_gate_up, b_gate, gla_norm_g, w_gla_out, w_o, norm2_g, w_ffn_up, ffn_dw_w, ffn_dw_b, w_ffn_down, final_g):
    bsz, seq, d = x.shape
    assert d == D_MODEL and seq % TILE_MIX == 0 and seq % TILE_FFN == 0
    assert norm1_g.shape[0] == 1, "single layer"
    l = 0
    glow0 = OFF_R + GLA_VDIM
    w = w_in[l].astype(_BF16)
    w_in_r = jnp.concatenate(
        [w[:, :glow0], w[:, glow0 + GATE_RANK:],
         jnp.pad(w[:, glow0:glow0 + GATE_RANK], ((0, 0), (0, RANK_PAD - GATE_RANK)))],
        axis=1)
    w_gu = jnp.pad(w_gate_up[l], ((0, RANK_PAD - GATE_RANK), (0, 0))).astype(_BF16)
    row2 = lambda a: a.reshape(1, -1)
    lane_tiles = lambda a: a.reshape(a.shape[0], N_LANE_TILES, LANES).transpose(1, 0, 2)

    tiles_per_seq = seq // TILE_MIX
    n_tiles = bsz * tiles_per_seq
    x2 = x.reshape(bsz * seq, D_MODEL)
    mix_args = (
        row2(norm1_g[l]), w_in_r, lane_tiles(conv_dw_w[l]), lane_tiles(row2(conv_dw_b[l])), row2(conv_ln_g[l]),
        row2(conv_ln_b[l]), w_conv_out[l].astype(_BF16), w_gu, row2(b_gate[l]),
        row2(gla_norm_g[l]), w_gla_out[l].astype(_BF16), w_o[l].astype(_BF16),
        _chunk_tri(TILE_MIX))
    cur_spec = pl.BlockSpec((TILE_MIX, D_MODEL), lambda s: (jnp.minimum(s, n_tiles - 1), 0))
    prev_spec = pl.BlockSpec((TILE_MIX, D_MODEL), lambda s: (jnp.maximum(s - 1, 0), 0))
    h1 = pl.pallas_call(
        functools.partial(_mixer_kernel, tiles_per_seq=tiles_per_seq),
        grid=(n_tiles + 1,),
        in_specs=[cur_spec, prev_spec] + [_resident(a.shape, 1) for a in mix_args],
        out_specs=prev_spec,
        out_shape=jax.ShapeDtypeStruct((bsz * seq, D_MODEL), _F32),
        scratch_shapes=[
            pltpu.VMEM((N_LANE_TILES, CONV_HALO + TILE_MIX, LANES), _F32),
            pltpu.VMEM((N_LANE_TILES, TILE_MIX, LANES), _F32),
            pltpu.VMEM((GLA_HEADS, GLA_DV, GLA_DK), _F32),
            pltpu.VMEM((TILE_MIX, GLA_VDIM), _BF16),
            pltpu.VMEM((TILE_MIX, D_MODEL), _BF16),
            pltpu.VMEM((TILE_MIX, D_MODEL), _BF16),
            pltpu.VMEM(((2 * GLA_KDIM + 2 * GLA_VDIM) // MXU_COLS, TILE_MIX, MXU_COLS), _F32),
            pltpu.VMEM((2 * D_MODEL // MXU_COLS, TILE_MIX, MXU_COLS), _F32),
        ],
        compiler_params=pltpu.CompilerParams(dimension_semantics=("arbitrary",),
                                             vmem_limit_bytes=VMEM_LIMIT),
        name="token_mixer",
    )(x2, x2, *mix_args).reshape(bsz, seq, D_MODEL)

    tile_spec = lambda t: pl.BlockSpec((None, t, D_MODEL), lambda b, s: (b, s, 0))
    ffn_args = (
        row2(norm2_g[l]), w_ffn_up[l].astype(_BF16), ffn_dw_w[l], row2(ffn_dw_b[l]),
        w_ffn_down[l].astype(_BF16), row2(final_g))
    return pl.pallas_call(
        _ffn_kernel,
        grid=(bsz, seq // TILE_FFN),
        in_specs=[tile_spec(TILE_FFN)] + [_resident(a.shape, 2) for a in ffn_args],
        out_specs=tile_spec(TILE_FFN),
        out_shape=jax.ShapeDtypeStruct((bsz, seq, D_MODEL), _F32),
        scratch_shapes=[pltpu.VMEM((FFN_HALO + TILE_FFN, 2 * FFN_DIM), _F32)],
        compiler_params=pltpu.CompilerParams(dimension_semantics=("arbitrary", "arbitrary"),
                                             vmem_limit_bytes=VMEM_LIMIT),
        name="channel_mixer",
    )(h1, *ffn_args)
```

```python
import functools

import jax
import jax.numpy as jnp
import numpy as np
from jax import lax
from jax.experimental import pallas as pl
from jax.experimental.pallas import tpu as pltpu

D_MODEL = 1024
CONV_CH = D_MODEL
CONV_WIDTH = 31
GLA_HEADS = 4
GLA_DK = 128
GLA_DV = 256
GLA_KDIM = GLA_HEADS * GLA_DK
GLA_VDIM = GLA_HEADS * GLA_DV
GATE_RANK = 16
GATE_NORMALIZER = 16.0
GLA_CHUNK = 64
FFN_DIM = 2816
FFN_CONV_WIDTH = 3
EPS = 1e-6

LANES = 128
SUBLANES = 8
RANK_PAD = LANES

IN_SEGMENTS = (CONV_CH, CONV_CH, GLA_KDIM, GLA_KDIM, GLA_VDIM, GLA_VDIM, GATE_RANK, D_MODEL, D_MODEL)

CONV_HALO = 32
FFN_HALO = SUBLANES

TILE_MIX = 256
CONV_COLS = 256
CONV_ROWS = 128
TILE_FFN = 512
VMEM_LIMIT = 56 * 1024 * 1024

_BF16 = jnp.bfloat16
_F32 = jnp.float32


def _mm(a, b):
    return jnp.dot(a, b, preferred_element_type=_F32)


def _mm_nt(a, b):
    return lax.dot_general(a, b, (((1,), (1,)), ((), ())), preferred_element_type=_F32)


def _mm_tn(a, b):
    return lax.dot_general(a, b, (((0,), (0,)), ((), ())), preferred_element_type=_F32)


def _sigmoid(x):
    return 1.0 / (1.0 + jnp.exp(-x))


def _silu(x):
    return x * _sigmoid(x)


def _log_sigmoid(x):
    return jnp.minimum(x, 0.0) - jnp.log1p(jnp.exp(-jnp.abs(x)))


def _rmsnorm(x, g):
    return x * lax.rsqrt(jnp.mean(x * x, axis=-1, keepdims=True) + EPS) * g


def _split3_bf16(x):
    hi = x.astype(_BF16)
    r1 = x - hi.astype(_F32)
    mid = r1.astype(_BF16)
    lo = (r1 - mid.astype(_F32)).astype(_BF16)
    return hi, mid, lo


def _conv_taps_by_phase():
    base = CONV_HALO - (CONV_WIDTH - 1)
    groups = {}
    for j in range(CONV_WIDTH):
        groups.setdefault((base + j) % SUBLANES, []).append(j)
    return base, sorted(groups.items())


def _conv_rows(cext_ref, cw_ref, cb_ref, conv_ref, lanes, r0):
    base, groups = _conv_taps_by_phase()
    out = None
    for phase, taps in groups:
        extra = SUBLANES if phase else 0
        z = None
        for j in taps:
            a0 = r0 + base + j - phase
            term = cw_ref[j:j + 1, lanes] * cext_ref[a0:a0 + CONV_ROWS + extra, lanes]
            z = term if z is None else z + term
        zs = z[phase:phase + CONV_ROWS] if phase else z
        out = zs if out is None else out + zs
    conv_ref[r0:r0 + CONV_ROWS, lanes] = out + cb_ref[:, lanes]


def _mixer_kernel(x_ref, xp_ref, n1g_ref, wa_ref, wg_ref, wq_ref, wk_ref, wv_ref, wr_ref, wga_ref,
                  wgb_ref, wlow_ref, cw_ref, cb_ref, lng_ref, lnb_ref, wco_ref,
                  wgu_ref, bg_ref, gng_ref, wgo_ref, wo_ref, tri_ref,
                  out_ref, cext_ref, conv_ref, st_ref, obuf_ref, u_ref, *, tiles_per_seq):
    tm = x_ref.shape[0]
    step = pl.program_id(0)

    @pl.when(step == 0)
    def _():
        conv_ref[...] = jnp.zeros(conv_ref.shape, _F32)
        obuf_ref[...] = jnp.zeros(obuf_ref.shape, _BF16)
        u_ref[...] = jnp.zeros(u_ref.shape, _BF16)

    @pl.when(step % tiles_per_seq == 0)
    def _():
        cext_ref[0:CONV_HALO, :] = jnp.zeros((CONV_HALO, CONV_CH), _F32)
        st_ref[...] = jnp.zeros(st_ref.shape, _F32)

    u = _rmsnorm(x_ref[...], n1g_ref[...]).astype(_BF16)
    u_prev = u_ref[...]

    def glu_block(cb):
        cols = slice(cb * CONV_COLS, (cb + 1) * CONV_COLS)
        a = _mm(u, wa_ref[:, cols])
        g = _mm(u, wg_ref[:, cols])
        cext_ref[CONV_HALO:CONV_HALO + tm, cols] = a * _sigmoid(g)

    conv_pieces = []
    for cb in range(CONV_CH // CONV_COLS):
        for lt in range(CONV_COLS // LANES):
            l0 = cb * CONV_COLS + lt * LANES
            for rb in range(tm // CONV_ROWS):
                conv_pieces.append((slice(l0, l0 + LANES), rb))

    def conv_piece(i):
        lanes, rb = conv_pieces[i]
        _conv_rows(cext_ref, cw_ref, cb_ref, conv_ref, lanes, rb * CONV_ROWS)
        if rb == tm // CONV_ROWS - 1:
            cext_ref[0:CONV_HALO, lanes] = cext_ref[tm:tm + CONV_HALO, lanes]

    y_gla = _mm(obuf_ref[...], wgo_ref[...])
    gate_a = _sigmoid(_mm(u_prev, wga_ref[...]))
    gate_b = _sigmoid(_mm(u_prev, wgb_ref[...]))

    acc = conv_ref[...]
    mu = jnp.mean(acc, axis=-1, keepdims=True)
    xc = acc - mu
    var = jnp.mean(xc * xc, axis=-1, keepdims=True)
    cn = _silu(xc * lax.rsqrt(var + EPS) * lng_ref[...] + lnb_ref[...]).astype(_BF16)

    g_low = _mm(u, wlow_ref[...]).astype(_BF16)
    glu_block(0)
    glu_block(1)
    gk_pre = _mm(g_low, wgu_ref[...]) + bg_ref[...]
    q = _mm(u, wq_ref[...])
    k = _mm(u, wk_ref[...])
    for i in range(0, 4):
        conv_piece(i)
    v = _mm(u, wv_ref[...]).astype(_BF16)
    glu_block(2)
    for i in range(4, 8):
        conv_piece(i)
    glu_block(3)
    r = _mm(u, wr_ref[...])

    gk = _log_sigmoid(gk_pre) * (1.0 / GATE_NORMALIZER)
    tri = tri_ref[...]
    g_hi, g_mid, g_lo = _split3_bf16(gk)
    b = _mm(tri, g_hi) + _mm(tri, g_mid) + _mm(tri, g_lo)

    y_conv = _mm(cn, wco_ref[...])
    for i in range(8, 12):
        conv_piece(i)

    n_chunks = tm // GLA_CHUNK
    row = lax.broadcasted_iota(jnp.int32, (GLA_CHUNK, GLA_CHUNK), 0)
    col = lax.broadcasted_iota(jnp.int32, (GLA_CHUNK, GLA_CHUNK), 1)
    causal = col <= row
    gng = gng_ref[...]
    scale = GLA_DK ** -0.5
    q_dec, k_inv, k_end, decay = [], [], [], []
    for ci in range(n_chunks):
        rows = slice(ci * GLA_CHUNK, (ci + 1) * GLA_CHUNK)
        b_c = b[rows]
        b_last = b_c[GLA_CHUNK - 1:GLA_CHUNK]
        q_dec.append((q[rows] * scale * jnp.exp(b_c)).astype(_BF16))
        k_inv.append((k[rows] * jnp.exp(-b_c)).astype(_BF16))
        k_end.append((k[rows] * jnp.exp(b_last - b_c)).astype(_BF16))
        decay.append(jnp.exp(b_last))
    head_k = [slice(h * GLA_DK, (h + 1) * GLA_DK) for h in range(GLA_HEADS)]
    head_v = [slice(h * GLA_DV, (h + 1) * GLA_DV) for h in range(GLA_HEADS)]
    units = [(ci, h) for ci in range(n_chunks) for h in range(GLA_HEADS)]
    att = {}
    for ci, h in units:
        s = _mm_nt(q_dec[ci][:, head_k[h]], k_inv[ci][:, head_k[h]])
        att[ci, h] = jnp.where(causal, s, 0.0).astype(_BF16)
    kv = {}
    for ci, h in units:
        rows = slice(ci * GLA_CHUNK, (ci + 1) * GLA_CHUNK)
        kv[ci, h] = _mm_tn(v[rows, head_v[h]], k_end[ci][:, head_k[h]])

    merged = (gate_a * y_conv + gate_b * y_gla).astype(_BF16)
    out_ref[...] = xp_ref[...] + _mm(merged, wo_ref[...])

    state = {}
    for h in range(GLA_HEADS):
        state[0, h] = st_ref[h]
        for ci in range(n_chunks):
            state[ci + 1, h] = decay[ci][:, head_k[h]] * state[ci, h] + kv[ci, h]
        st_ref[h] = state[n_chunks, h]
    for i in range(12, 16):
        conv_piece(i)
    for ci in range(n_chunks):
        rows = slice(ci * GLA_CHUNK, (ci + 1) * GLA_CHUNK)
        outs = []
        for h in range(GLA_HEADS):
            o = (_mm(att[ci, h], v[rows, head_v[h]])
                 + _mm_nt(q_dec[ci][:, head_k[h]], state[ci, h].astype(_BF16)))
            outs.append(_rmsnorm(o, gng))
        o_c = jnp.concatenate(outs, axis=1) * _silu(r[rows])
        obuf_ref[rows, :] = o_c.astype(_BF16)
    u_ref[...] = u


def _ffn_kernel(h_ref, n2g_ref, wup_ref, fw_ref, fb_ref, wdn_ref, fg_ref,
                out_ref, zext_ref):
    tm = h_ref.shape[0]

    @pl.when(pl.program_id(1) == 0)
    def _():
        zext_ref[0:FFN_HALO, :] = jnp.zeros((FFN_HALO, 2 * FFN_DIM), _F32)

    h = h_ref[...]
    u = _rmsnorm(h, n2g_ref[...]).astype(_BF16)
    zext_ref[FFN_HALO:FFN_HALO + tm, :] = _mm(u, wup_ref[...])
    base = FFN_HALO - (FFN_CONV_WIDTH - 1)
    z = jnp.broadcast_to(fb_ref[...], (tm, 2 * FFN_DIM))
    for j in range(FFN_CONV_WIDTH):
        z = z + fw_ref[j:j + 1, :] * zext_ref[base + j:base + j + tm, :]
    zext_ref[0:FFN_HALO, :] = zext_ref[tm:tm + FFN_HALO, :]
    act = (_silu(z[:, :FFN_DIM]) * z[:, FFN_DIM:]).astype(_BF16)
    h2 = h + _mm(act, wdn_ref[...])
    out_ref[...] = _rmsnorm(h2, fg_ref[...])


def _resident(shape, grid_rank):
    zeros = (0,) * len(shape)
    if grid_rank == 1:
        return pl.BlockSpec(shape, lambda s: zeros, pipeline_mode=pl.Buffered(1))
    return pl.BlockSpec(shape, lambda b, s: zeros, pipeline_mode=pl.Buffered(1))


def _chunk_tri(tile):
    idx = np.arange(tile)
    same = (idx[:, None] // GLA_CHUNK) == (idx[None, :] // GLA_CHUNK)
    return jnp.asarray((same & (idx[None, :] <= idx[:, None])).astype(np.float32), dtype=_BF16)


def kernel(x, norm1_g, w_in, conv_dw_w, conv_dw_b, conv_ln_g, conv_ln_b, w_conv_out, w_gate_up, b_gate, gla_norm_g, w_gla_out, w_o, norm2_g, w_ffn_up, ffn_dw_w, ffn_dw_b, w_ffn_down, final_g):
    bsz, seq, d = x.shape
    assert d == D_MODEL and seq % TILE_MIX == 0 and seq % TILE_FFN == 0
    assert norm1_g.shape[0] == 1, "single layer"
    l = 0
    bounds = [int(c) for c in np.cumsum((0,) + IN_SEGMENTS)]
    w_a, w_g, w_q, w_k, w_v, w_r, w_low, w_ga, w_gb = (
        w_in[l][:, lo:hi].astype(_BF16) for lo, hi in zip(bounds[:-1], bounds[1:]))
    w_low = jnp.pad(w_low, ((0, 0), (0, RANK_PAD - GATE_RANK)))
    w_gu = jnp.pad(w_gate_up[l], ((0, RANK_PAD - GATE_RANK), (0, 0))).astype(_BF16)
    row2 = lambda a: a.reshape(1, -1)

    tiles_per_seq = seq // TILE_MIX
    n_tiles = bsz * tiles_per_seq
    x2 = x.reshape(bsz * seq, D_MODEL)
    mix_args = (
        row2(norm1_g[l]), w_a, w_g, w_q, w_k, w_v, w_r, w_ga, w_gb, w_low,
        conv_dw_w[l], row2(conv_dw_b[l]), row2(conv_ln_g[l]),
        row2(conv_ln_b[l]), w_conv_out[l].astype(_BF16), w_gu, row2(b_gate[l]),
        row2(gla_norm_g[l]), w_gla_out[l].astype(_BF16), w_o[l].astype(_BF16),
        _chunk_tri(TILE_MIX))
    cur_spec = pl.BlockSpec((TILE_MIX, D_MODEL), lambda s: (jnp.minimum(s, n_tiles - 1), 0))
    prev_spec = pl.BlockSpec((TILE_MIX, D_MODEL), lambda s: (jnp.maximum(s - 1, 0), 0))
    h1 = pl.pallas_call(
        functools.partial(_mixer_kernel, tiles_per_seq=tiles_per_seq),
        grid=(n_tiles + 1,),
        in_specs=[cur_spec, prev_spec] + [_resident(a.shape, 1) for a in mix_args],
        out_specs=prev_spec,
        out_shape=jax.ShapeDtypeStruct((bsz * seq, D_MODEL), _F32),
        scratch_shapes=[
            pltpu.VMEM((CONV_HALO + TILE_MIX, CONV_CH), _F32),
            pltpu.VMEM((TILE_MIX, CONV_CH), _F32),
            pltpu.VMEM((GLA_HEADS, GLA_DV, GLA_DK), _F32),
            pltpu.VMEM((TILE_MIX, GLA_VDIM), _BF16),
            pltpu.VMEM((TILE_MIX, D_MODEL), _BF16),
        ],
        compiler_params=pltpu.CompilerParams(dimension_semantics=("arbitrary",),
                                             vmem_limit_bytes=VMEM_LIMIT),
        name="token_mixer",
    )(x2, x2, *mix_args).reshape(bsz, seq, D_MODEL)

    tile_spec = lambda t: pl.BlockSpec((None, t, D_MODEL), lambda b, s: (b, s, 0))
    ffn_args = (
        row2(norm2_g[l]), w_ffn_up[l].astype(_BF16), ffn_dw_w[l], row2(ffn_dw_b[l]),
        w_ffn_down[l].astype(_BF16), row2(final_g))
    return pl.pallas_call(
        _ffn_kernel,
        grid=(bsz, seq // TILE_FFN),
        in_specs=[tile_spec(TILE_FFN)] + [_resident(a.shape, 2) for a in ffn_args],
        out_specs=tile_spec(TILE_FFN),
        out_shape=jax.ShapeDtypeStruct((bsz, seq, D_MODEL), _F32),
        scratch_shapes=[pltpu.VMEM((FFN_HALO + TILE_FFN, 2 * FFN_DIM), _F32)],
        compiler_params=pltpu.CompilerParams(dimension_semantics=("arbitrary", "arbitrary"),
                                             vmem_limit_bytes=VMEM_LIMIT),
        name="channel_mixer",
    )(h1, *ffn_args)
```
